```python
import math
import jax, jax.numpy as jnp
from jax import lax
import numpy as np

D_MODEL = 1024
BATCH = 16
SEQ = 2048
DEPTH = 4
DEC_BATCH = 32
DEC_SEQ = 16
PAST_LEN = 2048

CHUNK = 64
N_PAST_CHUNKS = 8
BAND = (N_PAST_CHUNKS + 1) * CHUNK
HEAD_DIM = 64
ATT_W = D_MODEL // 2
N_HEADS = ATT_W // HEAD_DIM
CONV_CH = D_MODEL - ATT_W
N_CONV_GROUPS = CONV_CH // HEAD_DIM
MIX_W = ATT_W + CONV_CH
PROJ_W = 3 * ATT_W + 3 * CONV_CH
CONV_W = 3
REL_CLIP = 128
D_FF = ((8 * D_MODEL // 3 + 127) // 128) * 128
EPS = 1e-6
NEG_INF = -1e30
SCALE = HEAD_DIM ** -0.5

kernel_name = "hybrid_streaming_encoder_step"


def rmsnorm(x, g):
    xf = x.astype(jnp.float32)
    y = xf * lax.rsqrt(jnp.mean(xf * xf, axis=-1, keepdims=True) + EPS)
    return (y * g.astype(jnp.float32)).astype(x.dtype)


def group_rmsnorm(x, g, groups):
    shp = x.shape
    xf = x.astype(jnp.float32).reshape(shp[:-1] + (groups, shp[-1] // groups))
    y = xf * lax.rsqrt(jnp.mean(xf * xf, axis=-1, keepdims=True) + EPS)
    return (y.reshape(shp) * g.astype(jnp.float32)).astype(x.dtype)


def causal_dwconv(u_ext, w):
    T = u_ext.shape[1] - (CONV_W - 1)
    out = w[0] * u_ext[:, 0:T]
    for i in range(1, CONV_W):
        out = out + w[i] * u_ext[:, i:i + T]
    return out


def rel_bias(table, q_off, k_off):
    idx = jnp.clip(q_off[:, None] - k_off[None, :], -REL_CLIP, REL_CLIP) + REL_CLIP
    return table[:, idx]


def band_softmax(qc, kb, vb, bias, mask):
    s = jnp.einsum('bcqhd,bckhd->bchqk', qc, kb).astype(jnp.float32) + bias.astype(jnp.float32)
    s = jnp.where(mask[None, :, None, None, :], s, NEG_INF)
    p = jax.nn.softmax(s, axis=-1).astype(vb.dtype)
    return jnp.einsum('bchqk,bckhd->bcqhd', p, vb)


def chunk_band_attention(q, k, v, table):
    Bn, T, H, Dh = q.shape
    nc = T // CHUNK
    qc = (q * SCALE).reshape(Bn, nc, CHUNK, H, Dh)
    pad = jnp.zeros((Bn, N_PAST_CHUNKS, CHUNK, H, Dh), k.dtype)

    def band(t):
        tp = jnp.concatenate([pad, t.reshape(Bn, nc, CHUNK, H, Dh)], axis=1)
        return jnp.concatenate([tp[:, s:s + nc] for s in range(N_PAST_CHUNKS + 1)], axis=2)

    kb, vb = band(k), band(v)
    k_off = jnp.arange(BAND) - N_PAST_CHUNKS * CHUNK
    bias = rel_bias(table, jnp.arange(CHUNK), k_off)
    slot_ok = (jnp.arange(nc)[:, None] + jnp.arange(N_PAST_CHUNKS + 1)[None, :]) >= N_PAST_CHUNKS
    mask = jnp.repeat(slot_ok, CHUNK, axis=1)
    out = band_softmax(qc, kb, vb, bias, mask)
    return out.reshape(Bn, T, H * Dh)


def cached_band_attention(q, k, v, table, ck, cv):
    Bn, Tn, H, Dh = q.shape
    Lc = ck.shape[1]
    kb = jnp.concatenate([ck, k], axis=1)[:, None]
    vb = jnp.concatenate([cv, v], axis=1)[:, None]
    k_off = jnp.arange(Lc + Tn) - Lc
    bias = rel_bias(table, jnp.arange(Tn), k_off)
    mask = jnp.ones((1, Lc + Tn), dtype=bool)
    out = band_softmax((q * SCALE)[:, None], kb, vb, bias, mask)
    return out.reshape(Bn, Tn, H * Dh)


def trunk_layer(x, attn_fn, conv_hist, ffn_hist, ln1, w_in, rel_table, conv_w, attn_g, conv_g,
                w_out, ln2, w_up, fconv_w, fconv_b, w_down):
    Bn, T, _ = x.shape
    h = rmsnorm(x, ln1)
    proj = h @ w_in
    q = proj[..., 0:ATT_W].reshape(Bn, T, N_HEADS, HEAD_DIM)
    k = proj[..., ATT_W:2 * ATT_W].reshape(Bn, T, N_HEADS, HEAD_DIM)
    v = proj[..., 2 * ATT_W:3 * ATT_W].reshape(Bn, T, N_HEADS, HEAD_DIM)
    o = 3 * ATT_W
    bg = proj[..., o:o + CONV_CH]
    cg = proj[..., o + CONV_CH:o + 2 * CONV_CH]
    hv = proj[..., o + 2 * CONV_CH:o + 3 * CONV_CH]
    att = attn_fn(q, k, v, rel_table)
    u_ext = jnp.concatenate([conv_hist, cg * hv], axis=1)
    z = bg * causal_dwconv(u_ext, conv_w)
    mixed = jnp.concatenate([group_rmsnorm(att, attn_g, N_HEADS),
                             group_rmsnorm(z, conv_g, N_CONV_GROUPS)], axis=-1) @ w_out
    x = x + mixed
    up = rmsnorm(x, ln2) @ w_up
    up_ext = jnp.concatenate([ffn_hist, up], axis=1)
    a = causal_dwconv(up_ext, fconv_w) + fconv_b
    x = x + (jax.nn.silu(a[..., :D_FF]) * a[..., D_FF:]) @ w_down
    return x, k, v, u_ext[:, -(CONV_W - 1):], up_ext[:, -(CONV_W - 1):]


def setup_inputs(seed: int = 0) -> dict:
    key = jax.random.key(seed)
    ks = jax.random.split(key, 24)
    f32 = jnp.float32
    att_cache = min(N_PAST_CHUNKS * CHUNK, PAST_LEN)
    n = lambda i, shape, s: jax.random.normal(ks[i], shape, f32) * s
    return {
        "x_prompt": n(0, (BATCH, SEQ, D_MODEL), 1.0),
        "x_sample": n(1, (DEC_BATCH, DEC_SEQ, D_MODEL), 1.0),
        "cache_attn_k": n(2, (DEPTH, DEC_BATCH, att_cache, N_HEADS, HEAD_DIM), 1.0),
        "cache_attn_v": n(3, (DEPTH, DEC_BATCH, att_cache, N_HEADS, HEAD_DIM), 1.0),
        "state_mix_conv": n(4, (DEPTH, DEC_BATCH, CONV_W - 1, CONV_CH), 1.0),
        "state_ffn_conv": n(5, (DEPTH, DEC_BATCH, CONV_W - 1, 2 * D_FF), 1.0),
        "ln1": 1.0 + n(6, (DEPTH, D_MODEL), 0.02),
        "w_in": n(7, (DEPTH, D_MODEL, PROJ_W), D_MODEL ** -0.5),
        "rel_table": n(8, (DEPTH, N_HEADS, 2 * REL_CLIP + 1), 0.2),
        "conv_w": n(9, (DEPTH, CONV_W, CONV_CH), CONV_W ** -0.5),
        "attn_g": 1.0 + n(10, (DEPTH, ATT_W), 0.02),
        "conv_g": 1.0 + n(11, (DEPTH, CONV_CH), 0.02),
        "w_out": n(12, (DEPTH, MIX_W, D_MODEL), MIX_W ** -0.5),
        "ln2": 1.0 + n(13, (DEPTH, D_MODEL), 0.02),
        "w_up": n(14, (DEPTH, D_MODEL, 2 * D_FF), D_MODEL ** -0.5),
        "fconv_w": n(15, (DEPTH, CONV_W, 2 * D_FF), CONV_W ** -0.5),
        "fconv_b": n(16, (DEPTH, 2 * D_FF), 0.01),
        "w_down": n(17, (DEPTH, D_FF, D_MODEL), D_FF ** -0.5),
        "final_norm": 1.0 + n(18, (D_MODEL,), 0.02),
    }


def reference(x_prompt, x_sample, cache_attn_k, cache_attn_v, state_mix_conv, state_ffn_conv,
              ln1, w_in, rel_table, conv_w, attn_g, conv_g, w_out, ln2, w_up, fconv_w, fconv_b,
              w_down, final_norm):
    Bp, Tp, _ = x_prompt.shape
    keep = min(N_PAST_CHUNKS * CHUNK, Tp)
    xp, xs = x_prompt, x_sample
    kp_l, vp_l, cp_l, fp_l = [], [], [], []
    ks_l, vs_l, cs_l, fs_l = [], [], [], []
    for l in range(DEPTH):
        w = (ln1[l], w_in[l], rel_table[l], conv_w[l], attn_g[l], conv_g[l], w_out[l],
             ln2[l], w_up[l], fconv_w[l], fconv_b[l], w_down[l])
        zc = jnp.zeros((Bp, CONV_W - 1, CONV_CH), xp.dtype)
        zf = jnp.zeros((Bp, CONV_W - 1, 2 * D_FF), xp.dtype)
        xp, kp, vp, cp, fp = trunk_layer(xp, chunk_band_attention, zc, zf, *w)
        kp_l.append(kp[:, Tp - keep:])
        vp_l.append(vp[:, Tp - keep:])
        cp_l.append(cp)
        fp_l.append(fp)
        ck, cv = cache_attn_k[l], cache_attn_v[l]
        samp_attn = lambda q, k, v, t, ck=ck, cv=cv: cached_band_attention(q, k, v, t, ck, cv)
        xs, kk, vv, cs, fs = trunk_layer(xs, samp_attn, state_mix_conv[l], state_ffn_conv[l], *w)
        ks_l.append(kk)
        vs_l.append(vv)
        cs_l.append(cs)
        fs_l.append(fs)
    y_prompt = rmsnorm(xp, final_norm)
    y_sample = rmsnorm(xs, final_norm)
    return (y_prompt, y_sample,
            jnp.stack(kp_l), jnp.stack(vp_l), jnp.stack(cp_l), jnp.stack(fp_l),
            jnp.stack(ks_l), jnp.stack(vs_l), jnp.stack(cs_l), jnp.stack(fs_l))
```

```python
import functools

import numpy as np
import jax
import jax.numpy as jnp
from jax import lax
from jax.experimental import pallas as pl
from jax.experimental.pallas import tpu as pltpu

CHUNK = 64
N_PAST_CHUNKS = 8
HEAD_DIM = 64
REL_CLIP = 128
CONV_W = 3
EPS = 1e-6
NEG_INF = -1e30
SCALE = HEAD_DIM ** -0.5

TILE = N_PAST_CHUNKS * CHUNK
BAND = (N_PAST_CHUNKS + 1) * CHUNK
LANES = 128
HEADS_PER_GROUP = LANES // HEAD_DIM
FF_CHUNK = 256
VMEM_LIMIT = 56 * 1024 * 1024

F32 = jnp.float32
BF16 = jnp.bfloat16


def _resident(shape):
    zeros = (0,) * len(shape)
    return pl.BlockSpec(shape, lambda *_: zeros, pipeline_mode=pl.Buffered(1))


def _rmsnorm(x, g):
    ms = jnp.mean(x * x, axis=-1, keepdims=True)
    return x * lax.rsqrt(ms + EPS) * g


def _group_rmsnorm(x, gmat, g):
    ms = jnp.dot((x * x).astype(BF16), gmat, preferred_element_type=F32)
    return x * lax.rsqrt(ms + EPS) * g


def _shift_rows(u, prev1, prev2):
    row = lax.broadcasted_iota(jnp.int32, u.shape, 0)
    u1 = jnp.where(row == 0, prev1, pltpu.roll(u, 1, 0))
    u2 = jnp.where(row == 0, prev2, jnp.where(row == 1, prev1, pltpu.roll(u, 2, 0)))
    return u1, u2


def _shift_rows_streams(u, hist, period):
    row = lax.broadcasted_iota(jnp.int32, u.shape, 0) % period
    n = u.shape[0]
    u1 = jnp.where(row == 0, pltpu.roll(hist, n - 1, 0), pltpu.roll(u, 1, 0))
    u2 = jnp.where(row < 2, hist, pltpu.roll(u, 2, 0))
    return u1, u2


def _pair_queries(qc):
    lane = lax.broadcasted_iota(jnp.int32, qc.shape, 1)
    zero = jnp.zeros_like(qc)
    return jnp.concatenate([jnp.where(lane < HEAD_DIM, qc, zero),
                            jnp.where(lane >= HEAD_DIM, qc, zero)], axis=0)


def _unpair_output(o2):
    r = o2.shape[0] // 2
    lane = lax.broadcasted_iota(jnp.int32, (r, LANES), 1)
    return jnp.where(lane < HEAD_DIM, o2[:r], o2[r:])


def _qk(qq, kb):
    return lax.dot_general(qq, kb, (((1,), (1,)), ((), ())), preferred_element_type=F32)


def _prompt_mixer_kernel(x_ref, ln1_ref, win_ref, bias_ref, convw_ref, ag_ref, cg_ref, gmat_ref, wout_ref,
                         x1_ref, kout_ref, vout_ref, cst_ref,
                         q_buf, k_buf, v_buf, u_carry, att_buf):
    s = pl.program_id(1)
    last = pl.num_programs(1) - 1
    aw = q_buf.shape[1]
    n_groups = aw // LANES

    @pl.when(s == 0)
    def _():
        k_buf[0:TILE, :] = jnp.zeros((TILE, aw), BF16)
        v_buf[0:TILE, :] = jnp.zeros((TILE, aw), BF16)
        u_carry[...] = jnp.zeros(u_carry.shape, F32)

    x = x_ref[0]
    h = _rmsnorm(x, ln1_ref[...]).astype(BF16)

    def proj(i):
        return jnp.dot(h, win_ref[:, i * aw:(i + 1) * aw], preferred_element_type=F32)

    q_buf[...] = (proj(0) * SCALE).astype(BF16)
    k = proj(1)
    v = proj(2)
    k_buf[TILE:2 * TILE, :] = k.astype(BF16)
    v_buf[TILE:2 * TILE, :] = v.astype(BF16)

    @pl.when(s == last)
    def _():
        kout_ref[0] = k
        vout_ref[0] = v

    lane_b = lax.broadcasted_iota(jnp.int32, (2 * CHUNK, BAND), 1)

    def chunk_body(c, carry):
        r0 = pl.multiple_of(c * CHUNK, CHUNK)
        first_valid = jnp.where(s == 0, TILE - c * CHUNK, 0)
        valid = lane_b >= first_valid
        for p in range(n_groups):
            cols = slice(p * LANES, (p + 1) * LANES)
            qq = _pair_queries(q_buf[pl.ds(r0, CHUNK), cols])
            sc = _qk(qq, k_buf[pl.ds(r0, BAND), cols]) + bias_ref[p]
            sc = jnp.where(valid, sc, NEG_INF)
            m = jnp.max(sc, axis=-1, keepdims=True)
            e = jnp.exp(sc - m)
            l = jnp.sum(e, axis=-1, keepdims=True)
            o2 = jnp.dot(e.astype(BF16), v_buf[pl.ds(r0, BAND), cols], preferred_element_type=F32)
            att_buf[pl.ds(r0, CHUNK), cols] = _unpair_output(o2 / l)
        return carry

    lax.fori_loop(0, TILE // CHUNK, chunk_body, 0)

    bg = proj(3)
    u = proj(4) * proj(5)
    cst_ref[0] = u[TILE - (CONV_W - 1):, :]
    u1, u2 = _shift_rows(u, u_carry[7:8, :], u_carry[6:7, :])
    z = bg * (convw_ref[0:1, :] * u2 + convw_ref[1:2, :] * u1 + convw_ref[2:3, :] * u)
    u_carry[...] = u[TILE - 8:, :]

    gmat = gmat_ref[...]
    att_n = _group_rmsnorm(att_buf[...], gmat, ag_ref[...]).astype(BF16)
    z_n = _group_rmsnorm(z, gmat, cg_ref[...]).astype(BF16)
    mixed = (jnp.dot(att_n, wout_ref[0:aw, :], preferred_element_type=F32)
             + jnp.dot(z_n, wout_ref[aw:, :], preferred_element_type=F32))
    x1_ref[0] = x + mixed

    k_buf[0:TILE, :] = k_buf[TILE:2 * TILE, :]
    v_buf[0:TILE, :] = v_buf[TILE:2 * TILE, :]


def _prompt_mixer(x, ln1, w_in, bias, conv_w, attn_g, conv_g, gmat, w_out):
    nb, seq, d = x.shape
    aw = attn_g.shape[-1]
    cw = conv_g.shape[-1]
    assert seq % TILE == 0 and aw == cw and aw % LANES == 0
    grid = (nb, seq // TILE)
    return pl.pallas_call(
        _prompt_mixer_kernel,
        grid=grid,
        in_specs=[
            pl.BlockSpec((1, TILE, d), lambda b, s: (b, s, 0)),
            _resident(ln1.shape), _resident(w_in.shape), _resident(bias.shape), _resident(conv_w.shape),
            _resident(attn_g.shape), _resident(conv_g.shape), _resident(gmat.shape), _resident(w_out.shape),
        ],
        out_specs=[
            pl.BlockSpec((1, TILE, d), lambda b, s: (b, s, 0)),
            pl.BlockSpec((1, TILE, aw), lambda b, s: (b, 0, 0)),
            pl.BlockSpec((1, TILE, aw), lambda b, s: (b, 0, 0)),
            pl.BlockSpec((1, CONV_W - 1, cw), lambda b, s: (b, 0, 0)),
        ],
        out_shape=[
            jax.ShapeDtypeStruct((nb, seq, d), F32),
            jax.ShapeDtypeStruct((nb, TILE, aw), F32),
            jax.ShapeDtypeStruct((nb, TILE, aw), F32),
            jax.ShapeDtypeStruct((nb, CONV_W - 1, cw), F32),
        ],
        scratch_shapes=[
            pltpu.VMEM((TILE, aw), BF16),
            pltpu.VMEM((2 * TILE, aw), BF16),
            pltpu.VMEM((2 * TILE, aw), BF16),
            pltpu.VMEM((8, cw), F32),
            pltpu.VMEM((TILE, aw), F32),
        ],
        compiler_params=pltpu.CompilerParams(
            dimension_semantics=("arbitrary", "arbitrary"), vmem_limit_bytes=VMEM_LIMIT),
        name="prompt_mixer",
    )(x, ln1, w_in, bias, conv_w, attn_g, conv_g, gmat, w_out)


def _ffn_chunks(h, x, wup_ref, fcw_ref, fcb_ref, wdn_ref, shift, store_state):
    d_ff = wdn_ref.shape[0]
    acc = x
    for j in range(d_ff // FF_CHUNK):
        halves = []
        for c0 in (j * FF_CHUNK, d_ff + j * FF_CHUNK):
            cols = slice(c0, c0 + FF_CHUNK)
            up = jnp.dot(h, wup_ref[:, cols], preferred_element_type=F32)
            u1, u2 = shift(up, cols)
            store_state(up, cols)
            halves.append(fcw_ref[0:1, cols] * u2 + fcw_ref[1:2, cols] * u1 + fcw_ref[2:3, cols] * up
                          + fcb_ref[:, cols])
        a_g, a_v = halves
        act = (a_g * jax.nn.sigmoid(a_g) * a_v).astype(BF16)
        acc = acc + jnp.dot(act, wdn_ref[j * FF_CHUNK:(j + 1) * FF_CHUNK, :], preferred_element_type=F32)
    return acc


def _prompt_ffn_kernel(x_ref, ln2_ref, wup_ref, fcw_ref, fcb_ref, wdn_ref, fn_ref,
                       y_ref, fst_ref, up_carry, *, final_norm):
    s = pl.program_id(1)

    @pl.when(s == 0)
    def _():
        up_carry[...] = jnp.zeros(up_carry.shape, F32)

    x = x_ref[0]
    h = _rmsnorm(x, ln2_ref[...]).astype(BF16)

    def shift(up, cols):
        return _shift_rows(up, up_carry[7:8, cols], up_carry[6:7, cols])

    def store_state(up, cols):
        fst_ref[0, :, cols] = up[TILE - (CONV_W - 1):, :]
        up_carry[:, cols] = up[TILE - 8:, :]

    y = _ffn_chunks(h, x, wup_ref, fcw_ref, fcb_ref, wdn_ref, shift, store_state)
    if final_norm:
        y = _rmsnorm(y, fn_ref[...])
    y_ref[0] = y


def _prompt_ffn(x, ln2, w_up, fconv_w, fconv_b, w_down, fnorm, final_norm):
    nb, seq, d = x.shape
    f2 = w_up.shape[1]
    assert seq % TILE == 0 and (f2 // 2) % FF_CHUNK == 0
    return pl.pallas_call(
        functools.partial(_prompt_ffn_kernel, final_norm=final_norm),
        grid=(nb, seq // TILE),
        in_specs=[
            pl.BlockSpec((1, TILE, d), lambda b, s: (b, s, 0)),
            _resident(ln2.shape), _resident(w_up.shape), _resident(fconv_w.shape), _resident(fconv_b.shape),
            _resident(w_down.shape), _resident(fnorm.shape),
        ],
        out_specs=[
            pl.BlockSpec((1, TILE, d), lambda b, s: (b, s, 0)),
            pl.BlockSpec((1, CONV_W - 1, f2), lambda b, s: (b, 0, 0)),
        ],
        out_shape=[
            jax.ShapeDtypeStruct((nb, seq, d), F32),
            jax.ShapeDtypeStruct((nb, CONV_W - 1, f2), F32),
        ],
        scratch_shapes=[pltpu.VMEM((8, f2), F32)],
        compiler_params=pltpu.CompilerParams(
            dimension_semantics=("arbitrary", "arbitrary"), vmem_limit_bytes=VMEM_LIMIT),
        name="prompt_ffn",
    )(x, ln2, w_up, fconv_w, fconv_b, w_down, fnorm)


def _sample_mixer_kernel(x_ref, ln1_ref, win_ref, ck_ref, cv_ref, bias_c_ref, bias_n_ref, hist_ref, convw_ref,
                         ag_ref, cg_ref, gmat_ref, wout_ref,
                         x1_ref, kout_ref, vout_ref, cst_ref,
                         q_buf, k_buf, v_buf, att_buf, z_buf, *, t_new):
    b = pl.program_id(0)
    last = pl.num_programs(0) - 1
    aw = q_buf.shape[1]
    n_groups = aw // LANES

    @pl.when(b == 0)
    def _():
        h = _rmsnorm(x_ref[...], ln1_ref[...]).astype(BF16)

        def proj(i):
            return jnp.dot(h, win_ref[:, i * aw:(i + 1) * aw], preferred_element_type=F32)

        q_buf[...] = (proj(0) * SCALE).astype(BF16)
        k = proj(1)
        v = proj(2)
        kout_ref[...] = k
        vout_ref[...] = v
        k_buf[...] = k.astype(BF16)
        v_buf[...] = v.astype(BF16)
        bg = proj(3)
        u = proj(4) * proj(5)
        cst_ref[...] = u
        u1, u2 = _shift_rows_streams(u, hist_ref[...], t_new)
        z_buf[...] = bg * (convw_ref[0:1, :] * u2 + convw_ref[1:2, :] * u1 + convw_ref[2:3, :] * u)

    r0 = pl.multiple_of(b * t_new, t_new)
    ck = ck_ref[0].astype(BF16)
    cv = cv_ref[0].astype(BF16)
    for p in range(n_groups):
        cols = slice(p * LANES, (p + 1) * LANES)
        qq = _pair_queries(q_buf[pl.ds(r0, t_new), cols])
        sc_c = _qk(qq, ck[:, cols]) + bias_c_ref[p]
        sc_n = _qk(qq, k_buf[pl.ds(r0, t_new), cols]) + bias_n_ref[p]
        m = jnp.maximum(jnp.max(sc_c, axis=-1, keepdims=True), jnp.max(sc_n, axis=-1, keepdims=True))
        e_c = jnp.exp(sc_c - m)
        e_n = jnp.exp(sc_n - m)
        l = jnp.sum(e_c, axis=-1, keepdims=True) + jnp.sum(e_n, axis=-1, keepdims=True)
        o2 = (jnp.dot(e_c.astype(BF16), cv[:, cols], preferred_element_type=F32)
              + jnp.dot(e_n.astype(BF16), v_buf[pl.ds(r0, t_new), cols], preferred_element_type=F32))
        att_buf[pl.ds(r0, t_new), cols] = _unpair_output(o2 / l)

    @pl.when(b == last)
    def _():
        gmat = gmat_ref[...]
        att_n = _group_rmsnorm(att_buf[...], gmat, ag_ref[...]).astype(BF16)
        z_n = _group_rmsnorm(z_buf[...], gmat, cg_ref[...]).astype(BF16)
        mixed = (jnp.dot(att_n, wout_ref[0:aw, :], preferred_element_type=F32)
                 + jnp.dot(z_n, wout_ref[aw:, :], preferred_element_type=F32))
        x1_ref[...] = x_ref[...] + mixed


def _sample_mixer(x, ln1, w_in, ck, cv, bias_c, bias_n, hist, conv_w, attn_g, conv_g, gmat, w_out, t_new):
    n, d = x.shape
    nb, lc, aw = ck.shape
    assert n == nb * t_new
    full = lambda shape: pl.BlockSpec(shape, lambda b: (0,) * len(shape))
    return pl.pallas_call(
        functools.partial(_sample_mixer_kernel, t_new=t_new),
        grid=(nb,),
        in_specs=[
            _resident(x.shape), _resident(ln1.shape), _resident(w_in.shape),
            pl.BlockSpec((1, lc, aw), lambda b: (b, 0, 0)),
            pl.BlockSpec((1, lc, aw), lambda b: (b, 0, 0)),
            _resident(bias_c.shape), _resident(bias_n.shape), _resident(hist.shape), _resident(conv_w.shape),
            _resident(attn_g.shape), _resident(conv_g.shape), _resident(gmat.shape), _resident(w_out.shape),
        ],
        out_specs=[full((n, d)), full((n, aw)), full((n, aw)), full((n, aw))],
        out_shape=[
            jax.ShapeDtypeStruct((n, d), F32),
            jax.ShapeDtypeStruct((n, aw), F32),
            jax.ShapeDtypeStruct((n, aw), F32),
            jax.ShapeDtypeStruct((n, aw), F32),
        ],
        scratch_shapes=[
            pltpu.VMEM((n, aw), BF16), pltpu.VMEM((n, aw), BF16), pltpu.VMEM((n, aw), BF16),
            pltpu.VMEM((n, aw), F32), pltpu.VMEM((n, aw), F32),
        ],
        compiler_params=pltpu.CompilerParams(
            dimension_semantics=("arbitrary",), vmem_limit_bytes=VMEM_LIMIT),
        name="sample_mixer",
    )(x, ln1, w_in, ck, cv, bias_c, bias_n, hist, conv_w, attn_g, conv_g, gmat, w_out)


def _sample_ffn_kernel(x_ref, ln2_ref, wup_ref, hist_ref, fcw_ref, fcb_ref, wdn_ref, fn_ref,
                       y_ref, up_ref, *, t_new, final_norm):
    x = x_ref[...]
    h = _rmsnorm(x, ln2_ref[...]).astype(BF16)

    def shift(up, cols):
        return _shift_rows_streams(up, hist_ref[:, cols], t_new)

    def store_state(up, cols):
        up_ref[:, cols] = up

    y = _ffn_chunks(h, x, wup_ref, fcw_ref, fcb_ref, wdn_ref, shift, store_state)
    if final_norm:
        y = _rmsnorm(y, fn_ref[...])
    y_ref[...] = y


def _sample_ffn(x, ln2, w_up, hist, fconv_w, fconv_b, w_down, fnorm, t_new, final_norm):
    n, d = x.shape
    f2 = w_up.shape[1]
    args = (x, ln2, w_up, hist, fconv_w, fconv_b, w_down, fnorm)
    full = lambda shape: pl.BlockSpec(shape, lambda i: (0,) * len(shape))
    return pl.pallas_call(
        functools.partial(_sample_ffn_kernel, t_new=t_new, final_norm=final_norm),
        grid=(1,),
        in_specs=[_resident(a.shape) for a in args],
        out_specs=[full((n, d)), full((n, f2))],
        out_shape=[jax.ShapeDtypeStruct((n, d), F32), jax.ShapeDtypeStruct((n, f2), F32)],
        compiler_params=pltpu.CompilerParams(
            dimension_semantics=("arbitrary",), vmem_limit_bytes=VMEM_LIMIT),
        name="sample_ffn",
    )(*args)


def _rel_index(q_off, k_off):
    return np.clip(q_off[:, None] - k_off[None, :], -REL_CLIP, REL_CLIP) + REL_CLIP


def _paired_bias(table, idx):
    h = table.shape[0]
    bias = table[:, idx]
    return bias.reshape(h // HEADS_PER_GROUP, HEADS_PER_GROUP * idx.shape[0], idx.shape[1])


def _group_mean_matrix(width, group):
    g = np.arange(width) // group
    return jnp.asarray((g[:, None] == g[None, :]).astype(np.float32) / group, dtype=BF16)


def _stream_history(state, t_new):
    nb, hw, c = state.shape
    return jnp.pad(state, ((0, 0), (0, t_new - hw), (0, 0))).reshape(nb * t_new, c)


def kernel(x_prompt, x_sample, cache_attn_k, cache_attn_v, state_mix_conv, state_ffn_conv, ln1, w_in, rel_table,
           conv_w, attn_g, conv_g, w_out, ln2, w_up, fconv_w, fconv_b, w_down, final_norm):
    depth = ln1.shape[0]
    bp, tp, d = x_prompt.shape
    bs, ts, _ = x_sample.shape
    lc = cache_attn_k.shape[2]
    n_heads, head_dim = cache_attn_k.shape[3:]
    aw = n_heads * head_dim
    assert head_dim == HEAD_DIM and tp >= TILE and lc == TILE and ts >= CONV_W - 1

    idx_prompt = _rel_index(np.arange(CHUNK), np.arange(BAND) - N_PAST_CHUNKS * CHUNK)
    idx_cache = _rel_index(np.arange(ts), np.arange(lc) - lc)
    idx_new = _rel_index(np.arange(ts), np.arange(ts))
    gmat = _group_mean_matrix(aw, HEAD_DIM)
    fnorm = final_norm.reshape(1, d)

    xp = x_prompt
    xs = x_sample.reshape(bs * ts, d)
    outs = [[] for _ in range(8)]
    for l in range(depth):
        w_in_l = w_in[l].astype(BF16)
        w_out_l = w_out[l].astype(BF16)
        w_up_l = w_up[l].astype(BF16)
        w_down_l = w_down[l].astype(BF16)
        ln1_l, ln2_l = ln1[l].reshape(1, d), ln2[l].reshape(1, d)
        ag_l, cg_l = attn_g[l].reshape(1, -1), conv_g[l].reshape(1, -1)
        fcb_l = fconv_b[l].reshape(1, -1)
        is_last = l == depth - 1

        xp, kp, vp, cp = _prompt_mixer(xp, ln1_l, w_in_l, _paired_bias(rel_table[l], idx_prompt), conv_w[l],
                                       ag_l, cg_l, gmat, w_out_l)
        xp, fp = _prompt_ffn(xp, ln2_l, w_up_l, fconv_w[l], fcb_l, w_down_l, fnorm, is_last)

        xs, ks, vs, us = _sample_mixer(
            xs, ln1_l, w_in_l, cache_attn_k[l].reshape(bs, lc, aw), cache_attn_v[l].reshape(bs, lc, aw),
            _paired_bias(rel_table[l], idx_cache), _paired_bias(rel_table[l], idx_new),
            _stream_history(state_mix_conv[l], ts), conv_w[l], ag_l, cg_l, gmat, w_out_l, ts)
        xs, ups = _sample_ffn(xs, ln2_l, w_up_l, _stream_history(state_ffn_conv[l], ts), fconv_w[l], fcb_l,
                              w_down_l, fnorm, ts, is_last)

        new = (kp.reshape(bp, TILE, n_heads, head_dim), vp.reshape(bp, TILE, n_heads, head_dim), cp, fp,
               ks.reshape(bs, ts, n_heads, head_dim), vs.reshape(bs, ts, n_heads, head_dim),
               us.reshape(bs, ts, -1)[:, ts - (CONV_W - 1):], ups.reshape(bs, ts, -1)[:, ts - (CONV_W - 1):])
        for acc, val in zip(outs, new):
            acc.append(val)

    return (xp, xs.reshape(bs, ts, d)) + tuple(jnp.stack(o) for o in outs)
```

```python
import functools

import numpy as np
import jax
import jax.numpy as jnp
from jax import lax
from jax.experimental import pallas as pl
from jax.experimental.pallas import tpu as pltpu

CHUNK = 64
N_PAST_CHUNKS = 8
HEAD_DIM = 64
REL_CLIP = 128
CONV_W = 3
EPS = 1e-6
NEG_INF = -1e30
SCALE = HEAD_DIM ** -0.5

TILE = N_PAST_CHUNKS * CHUNK
BAND = (N_PAST_CHUNKS + 1) * CHUNK
Q_BLOCK = 2 * CHUNK
K_WINDOW = BAND + CHUNK
LANES = 128
HEADS_PER_GROUP = LANES // HEAD_DIM
FF_CHUNK = 256
VMEM_LIMIT = 56 * 1024 * 1024

F32 = jnp.float32
BF16 = jnp.bfloat16


def _resident(shape):
    zeros = (0,) * len(shape)
    return pl.BlockSpec(shape, lambda *_: zeros, pipeline_mode=pl.Buffered(1))


def _rmsnorm(x, g):
    ms = jnp.mean(x * x, axis=-1, keepdims=True)
    return x * lax.rsqrt(ms + EPS) * g


def _group_rmsnorm(x, gmat, g):
    ms = jnp.dot((x * x).astype(BF16), gmat, preferred_element_type=F32)
    return x * lax.rsqrt(ms + EPS) * g


def _shift_rows(u, prev1, prev2):
    row = lax.broadcasted_iota(jnp.int32, u.shape, 0)
    u1 = jnp.where(row == 0, prev1, pltpu.roll(u, 1, 0))
    u2 = jnp.where(row == 0, prev2, jnp.where(row == 1, prev1, pltpu.roll(u, 2, 0)))
    return u1, u2


def _shift_rows_streams(u, hist, period):
    row = lax.broadcasted_iota(jnp.int32, u.shape, 0) % period
    n = u.shape[0]
    u1 = jnp.where(row == 0, pltpu.roll(hist, n - 1, 0), pltpu.roll(u, 1, 0))
    u2 = jnp.where(row < 2, hist, pltpu.roll(u, 2, 0))
    return u1, u2


def _pair_queries(qc):
    lane = lax.broadcasted_iota(jnp.int32, qc.shape, 1)
    zero = jnp.zeros_like(qc)
    return jnp.concatenate([jnp.where(lane < HEAD_DIM, qc, zero),
                            jnp.where(lane >= HEAD_DIM, qc, zero)], axis=0)


def _unpair_output(o2):
    r = o2.shape[0] // 2
    lane = lax.broadcasted_iota(jnp.int32, (r, LANES), 1)
    return jnp.where(lane < HEAD_DIM, o2[:r], o2[r:])


def _qk(qq, kb):
    return lax.dot_general(qq, kb, (((1,), (1,)), ((), ())), preferred_element_type=F32)


def _prompt_mixer_kernel(x_ref, ln1_ref, win_ref, bias_ref, convw_ref, ag_ref, cg_ref, gmat_ref, wout_ref,
                         x1_ref, kout_ref, vout_ref, cst_ref,
                         q_buf, k_buf, v_buf, u_carry, att_buf):
    s = pl.program_id(1)
    last = pl.num_programs(1) - 1
    aw = q_buf.shape[1]
    n_groups = aw // LANES

    @pl.when(s == 0)
    def _():
        k_buf[0:TILE, :] = jnp.zeros((TILE, aw), BF16)
        v_buf[0:TILE, :] = jnp.zeros((TILE, aw), BF16)
        u_carry[...] = jnp.zeros(u_carry.shape, F32)

    x = x_ref[0]
    h = _rmsnorm(x, ln1_ref[...]).astype(BF16)

    def proj(i):
        return jnp.dot(h, win_ref[:, i * aw:(i + 1) * aw], preferred_element_type=F32)

    q_buf[...] = (proj(0) * SCALE).astype(BF16)
    k = proj(1)
    v = proj(2)
    k_buf[TILE:2 * TILE, :] = k.astype(BF16)
    v_buf[TILE:2 * TILE, :] = v.astype(BF16)

    @pl.when(s == last)
    def _():
        kout_ref[0] = k
        vout_ref[0] = v

    lane_w = lax.broadcasted_iota(jnp.int32, (2 * Q_BLOCK, K_WINDOW), 1)

    def block_body(i, carry):
        r0 = pl.multiple_of(i * Q_BLOCK, Q_BLOCK)
        first_valid = jnp.where(s == 0, TILE - i * Q_BLOCK, 0)
        valid = lane_w >= first_valid
        for p in range(n_groups):
            cols = slice(p * LANES, (p + 1) * LANES)
            qq = _pair_queries(q_buf[pl.ds(r0, Q_BLOCK), cols])
            sc = _qk(qq, k_buf[pl.ds(r0, K_WINDOW), cols]) + bias_ref[p]
            sc = jnp.where(valid, sc, NEG_INF)
            m = jnp.max(sc, axis=-1, keepdims=True)
            e = jnp.exp(sc - m)
            l = jnp.sum(e, axis=-1, keepdims=True)
            o2 = jnp.dot(e.astype(BF16), v_buf[pl.ds(r0, K_WINDOW), cols], preferred_element_type=F32)
            att_buf[pl.ds(r0, Q_BLOCK), cols] = _unpair_output(o2 / l)
        return carry

    lax.fori_loop(0, TILE // Q_BLOCK, block_body, 0)

    bg = proj(3)
    u = proj(4) * proj(5)
    cst_ref[0] = u[TILE - (CONV_W - 1):, :]
    u1, u2 = _shift_rows(u, u_carry[7:8, :], u_carry[6:7, :])
    z = bg * (convw_ref[0:1, :] * u2 + convw_ref[1:2, :] * u1 + convw_ref[2:3, :] * u)
    u_carry[...] = u[TILE - 8:, :]

    gmat = gmat_ref[...]
    att_n = _group_rmsnorm(att_buf[...], gmat, ag_ref[...]).astype(BF16)
    z_n = _group_rmsnorm(z, gmat, cg_ref[...]).astype(BF16)
    mixed = (jnp.dot(att_n, wout_ref[0:aw, :], preferred_element_type=F32)
             + jnp.dot(z_n, wout_ref[aw:, :], preferred_element_type=F32))
    x1_ref[0] = x + mixed

    k_buf[0:TILE, :] = k_buf[TILE:2 * TILE, :]
    v_buf[0:TILE, :] = v_buf[TILE:2 * TILE, :]


def _prompt_mixer(x, ln1, w_in, bias, conv_w, attn_g, conv_g, gmat, w_out):
    nb, seq, d = x.shape
    aw = attn_g.shape[-1]
    cw = conv_g.shape[-1]
    assert seq % TILE == 0 and aw == cw and aw % LANES == 0
    grid = (nb, seq // TILE)
    return pl.pallas_call(
        _prompt_mixer_kernel,
        grid=grid,
        in_specs=[
            pl.BlockSpec((1, TILE, d), lambda b, s: (b, s, 0)),
            _resident(ln1.shape), _resident(w_in.shape), _resident(bias.shape), _resident(conv_w.shape),
            _resident(attn_g.shape), _resident(conv_g.shape), _resident(gmat.shape), _resident(w_out.shape),
        ],
        out_specs=[
            pl.BlockSpec((1, TILE, d), lambda b, s: (b, s, 0)),
            pl.BlockSpec((1, TILE, aw), lambda b, s: (b, 0, 0)),
            pl.BlockSpec((1, TILE, aw), lambda b, s: (b, 0, 0)),
            pl.BlockSpec((1, CONV_W - 1, cw), lambda b, s: (b, 0, 0)),
        ],
        out_shape=[
            jax.ShapeDtypeStruct((nb, seq, d), F32),
            jax.ShapeDtypeStruct((nb, TILE, aw), F32),
            jax.ShapeDtypeStruct((nb, TILE, aw), F32),
            jax.ShapeDtypeStruct((nb, CONV_W - 1, cw), F32),
        ],
        scratch_shapes=[
            pltpu.VMEM((TILE, aw), BF16),
            pltpu.VMEM((2 * TILE, aw), BF16),
            pltpu.VMEM((2 * TILE, aw), BF16),
            pltpu.VMEM((8, cw), F32),
            pltpu.VMEM((TILE, aw), F32),
        ],
        compiler_params=pltpu.CompilerParams(
            dimension_semantics=("arbitrary", "arbitrary"), vmem_limit_bytes=VMEM_LIMIT),
        name="prompt_mixer",
    )(x, ln1, w_in, bias, conv_w, attn_g, conv_g, gmat, w_out)


def _ffn_chunks(h, x, wup_ref, fcw_ref, fcb_ref, wdn_ref, shift, store_state):
    d_ff = wdn_ref.shape[0]
    acc = x
    for j in range(d_ff // FF_CHUNK):
        halves = []
        for c0 in (j * FF_CHUNK, d_ff + j * FF_CHUNK):
            cols = slice(c0, c0 + FF_CHUNK)
            up = jnp.dot(h, wup_ref[:, cols], preferred_element_type=F32)
            u1, u2 = shift(up, cols)
            store_state(up, cols)
            halves.append(fcw_ref[0:1, cols] * u2 + fcw_ref[1:2, cols] * u1 + fcw_ref[2:3, cols] * up
                          + fcb_ref[:, cols])
        a_g, a_v = halves
        act = (a_g * jax.nn.sigmoid(a_g) * a_v).astype(BF16)
        acc = acc + jnp.dot(act, wdn_ref[j * FF_CHUNK:(j + 1) * FF_CHUNK, :], preferred_element_type=F32)
    return acc


def _prompt_ffn_kernel(x_ref, ln2_ref, wup_ref, fcw_ref, fcb_ref, wdn_ref, fn_ref,
                       y_ref, fst_ref, up_carry, *, final_norm):
    s = pl.program_id(1)

    @pl.when(s == 0)
    def _():
        up_carry[...] = jnp.zeros(up_carry.shape, F32)

    x = x_ref[0]
    h = _rmsnorm(x, ln2_ref[...]).astype(BF16)

    def shift(up, cols):
        return _shift_rows(up, up_carry[7:8, cols], up_carry[6:7, cols])

    def store_state(up, cols):
        fst_ref[0, :, cols] = up[TILE - (CONV_W - 1):, :]
        up_carry[:, cols] = up[TILE - 8:, :]

    y = _ffn_chunks(h, x, wup_ref, fcw_ref, fcb_ref, wdn_ref, shift, store_state)
    if final_norm:
        y = _rmsnorm(y, fn_ref[...])
    y_ref[0] = y


def _prompt_ffn(x, ln2, w_up, fconv_w, fconv_b, w_down, fnorm, final_norm):
    nb, seq, d = x.shape
    f2 = w_up.shape[1]
    assert seq % TILE == 0 and (f2 // 2) % FF_CHUNK == 0
    return pl.pallas_call(
        functools.partial(_prompt_ffn_kernel, final_norm=final_norm),
        grid=(nb, seq // TILE),
        in_specs=[
            pl.BlockSpec((1, TILE, d), lambda b, s: (b, s, 0)),
            _resident(ln2.shape), _resident(w_up.shape), _resident(fconv_w.shape), _resident(fconv_b.shape),
            _resident(w_down.shape), _resident(fnorm.shape),
        ],
        out_specs=[
            pl.BlockSpec((1, TILE, d), lambda b, s: (b, s, 0)),
            pl.BlockSpec((1, CONV_W - 1, f2), lambda b, s: (b, 0, 0)),
        ],
        out_shape=[
            jax.ShapeDtypeStruct((nb, seq, d), F32),
            jax.ShapeDtypeStruct((nb, CONV_W - 1, f2), F32),
        ],
        scratch_shapes=[pltpu.VMEM((8, f2), F32)],
        compiler_params=pltpu.CompilerParams(
            dimension_semantics=("arbitrary", "arbitrary"), vmem_limit_bytes=VMEM_LIMIT),
        name="prompt_ffn",
    )(x, ln2, w_up, fconv_w, fconv_b, w_down, fnorm)


def _sample_mixer_kernel(x_ref, ln1_ref, win_ref, ck_ref, cv_ref, bias_c_ref, bias_n_ref, hist_ref, convw_ref,
                         ag_ref, cg_ref, gmat_ref, wout_ref,
                         x1_ref, kout_ref, vout_ref, cst_ref,
                         q_buf, k_buf, v_buf, att_buf, z_buf, *, t_new):
    b = pl.program_id(0)
    last = pl.num_programs(0) - 1
    aw = q_buf.shape[1]
    n_groups = aw // LANES

    @pl.when(b == 0)
    def _():
        h = _rmsnorm(x_ref[...], ln1_ref[...]).astype(BF16)

        def proj(i):
            return jnp.dot(h, win_ref[:, i * aw:(i + 1) * aw], preferred_element_type=F32)

        q_buf[...] = (proj(0) * SCALE).astype(BF16)
        k = proj(1)
        v = proj(2)
        kout_ref[...] = k
        vout_ref[...] = v
        k_buf[...] = k.astype(BF16)
        v_buf[...] = v.astype(BF16)
        bg = proj(3)
        u = proj(4) * proj(5)
        cst_ref[...] = u
        u1, u2 = _shift_rows_streams(u, hist_ref[...], t_new)
        z_buf[...] = bg * (convw_ref[0:1, :] * u2 + convw_ref[1:2, :] * u1 + convw_ref[2:3, :] * u)

    r0 = pl.multiple_of(b * t_new, t_new)
    ck = ck_ref[0].astype(BF16)
    cv = cv_ref[0].astype(BF16)
    for p in range(n_groups):
        cols = slice(p * LANES, (p + 1) * LANES)
        qq = _pair_queries(q_buf[pl.ds(r0, t_new), cols])
        sc_c = _qk(qq, ck[:, cols]) + bias_c_ref[p]
        sc_n = _qk(qq, k_buf[pl.ds(r0, t_new), cols]) + bias_n_ref[p]
        m = jnp.maximum(jnp.max(sc_c, axis=-1, keepdims=True), jnp.max(sc_n, axis=-1, keepdims=True))
        e_c = jnp.exp(sc_c - m)
        e_n = jnp.exp(sc_n - m)
        l = jnp.sum(e_c, axis=-1, keepdims=True) + jnp.sum(e_n, axis=-1, keepdims=True)
        o2 = (jnp.dot(e_c.astype(BF16), cv[:, cols], preferred_element_type=F32)
              + jnp.dot(e_n.astype(BF16), v_buf[pl.ds(r0, t_new), cols], preferred_element_type=F32))
        att_buf[pl.ds(r0, t_new), cols] = _unpair_output(o2 / l)

    @pl.when(b == last)
    def _():
        gmat = gmat_ref[...]
        att_n = _group_rmsnorm(att_buf[...], gmat, ag_ref[...]).astype(BF16)
        z_n = _group_rmsnorm(z_buf[...], gmat, cg_ref[...]).astype(BF16)
        mixed = (jnp.dot(att_n, wout_ref[0:aw, :], preferred_element_type=F32)
                 + jnp.dot(z_n, wout_ref[aw:, :], preferred_element_type=F32))
        x1_ref[...] = x_ref[...] + mixed


def _sample_mixer(x, ln1, w_in, ck, cv, bias_c, bias_n, hist, conv_w, attn_g, conv_g, gmat, w_out, t_new):
    n, d = x.shape
    nb, lc, aw = ck.shape
    assert n == nb * t_new
    full = lambda shape: pl.BlockSpec(shape, lambda b: (0,) * len(shape))
    return pl.pallas_call(
        functools.partial(_sample_mixer_kernel, t_new=t_new),
        grid=(nb,),
        in_specs=[
            _resident(x.shape), _resident(ln1.shape), _resident(w_in.shape),
            pl.BlockSpec((1, lc, aw), lambda b: (b, 0, 0)),
            pl.BlockSpec((1, lc, aw), lambda b: (b, 0, 0)),
            _resident(bias_c.shape), _resident(bias_n.shape), _resident(hist.shape), _resident(conv_w.shape),
            _resident(attn_g.shape), _resident(conv_g.shape), _resident(gmat.shape), _resident(w_out.shape),
        ],
        out_specs=[full((n, d)), full((n, aw)), full((n, aw)), full((n, aw))],
        out_shape=[
            jax.ShapeDtypeStruct((n, d), F32),
            jax.ShapeDtypeStruct((n, aw), F32),
            jax.ShapeDtypeStruct((n, aw), F32),
            jax.ShapeDtypeStruct((n, aw), F32),
        ],
        scratch_shapes=[
            pltpu.VMEM((n, aw), BF16), pltpu.VMEM((n, aw), BF16), pltpu.VMEM((n, aw), BF16),
            pltpu.VMEM((n, aw), F32), pltpu.VMEM((n, aw), F32),
        ],
        compiler_params=pltpu.CompilerParams(
            dimension_semantics=("arbitrary",), vmem_limit_bytes=VMEM_LIMIT),
        name="sample_mixer",
    )(x, ln1, w_in, ck, cv, bias_c, bias_n, hist, conv_w, attn_g, conv_g, gmat, w_out)


def _sample_ffn_kernel(x_ref, ln2_ref, wup_ref, hist_ref, fcw_ref, fcb_ref, wdn_ref, fn_ref,
                       y_ref, up_ref, *, t_new, final_norm):
    x = x_ref[...]
    h = _rmsnorm(x, ln2_ref[...]).astype(BF16)

    def shift(up, cols):
        return _shift_rows_streams(up, hist_ref[:, cols], t_new)

    def store_state(up, cols):
        up_ref[:, cols] = up

    y = _ffn_chunks(h, x, wup_ref, fcw_ref, fcb_ref, wdn_ref, shift, store_state)
    if final_norm:
        y = _rmsnorm(y, fn_ref[...])
    y_ref[...] = y


def _sample_ffn(x, ln2, w_up, hist, fconv_w, fconv_b, w_down, fnorm, t_new, final_norm):
    n, d = x.shape
    f2 = w_up.shape[1]
    args = (x, ln2, w_up, hist, fconv_w, fconv_b, w_down, fnorm)
    full = lambda shape: pl.BlockSpec(shape, lambda i: (0,) * len(shape))
    return pl.pallas_call(
        functools.partial(_sample_ffn_kernel, t_new=t_new, final_norm=final_norm),
        grid=(1,),
        in_specs=[_resident(a.shape) for a in args],
        out_specs=[full((n, d)), full((n, f2))],
        out_shape=[jax.ShapeDtypeStruct((n, d), F32), jax.ShapeDtypeStruct((n, f2), F32)],
        compiler_params=pltpu.CompilerParams(
            dimension_semantics=("arbitrary",), vmem_limit_bytes=VMEM_LIMIT),
        name="sample_ffn",
    )(*args)


def _toeplitz_bias(table, n_q, n_k, offset):
    m = np.arange(n_q + n_k - 1)
    diag = table[:, np.clip(offset + n_q - 1 - m, -REL_CLIP, REL_CLIP) + REL_CLIP]
    return jnp.stack([diag[:, n_q - 1 - q:n_q - 1 - q + n_k] for q in range(n_q)], axis=1)


def _pair_rows(bias):
    h, q, k = bias.shape
    return bias.reshape(h // HEADS_PER_GROUP, HEADS_PER_GROUP * q, k)


def _prompt_bias(table):
    q = np.arange(Q_BLOCK)[:, None]
    k = np.arange(K_WINDOW)[None, :]
    in_band = np.where(q < CHUNK, k < BAND, k >= CHUNK)
    bias = _toeplitz_bias(table, Q_BLOCK, K_WINDOW, N_PAST_CHUNKS * CHUNK)
    return _pair_rows(jnp.where(in_band[None], bias, NEG_INF))


def _group_mean_matrix(width, group):
    g = np.arange(width) // group
    return jnp.asarray((g[:, None] == g[None, :]).astype(np.float32) / group, dtype=BF16)


def _stream_history(state, t_new):
    nb, hw, c = state.shape
    return jnp.pad(state, ((0, 0), (0, t_new - hw), (0, 0))).reshape(nb * t_new, c)


def kernel(x_prompt, x_sample, cache_attn_k, cache_attn_v, state_mix_conv, state_ffn_conv, ln1, w_in, rel_table,
           conv_w, attn_g, conv_g, w_out, ln2, w_up, fconv_w, fconv_b, w_down, final_norm):
    depth = ln1.shape[0]
    bp, tp, d = x_prompt.shape
    bs, ts, _ = x_sample.shape
    lc = cache_attn_k.shape[2]
    n_heads, head_dim = cache_attn_k.shape[3:]
    aw = n_heads * head_dim
    assert head_dim == HEAD_DIM and tp >= TILE and lc == TILE and ts >= CONV_W - 1

    gmat = _group_mean_matrix(aw, HEAD_DIM)
    fnorm = final_norm.reshape(1, d)

    xp = x_prompt
    xs = x_sample.reshape(bs * ts, d)
    outs = [[] for _ in range(8)]
    for l in range(depth):
        w_in_l = w_in[l].astype(BF16)
        w_out_l = w_out[l].astype(BF16)
        w_up_l = w_up[l].astype(BF16)
        w_down_l = w_down[l].astype(BF16)
        ln1_l, ln2_l = ln1[l].reshape(1, d), ln2[l].reshape(1, d)
        ag_l, cg_l = attn_g[l].reshape(1, -1), conv_g[l].reshape(1, -1)
        fcb_l = fconv_b[l].reshape(1, -1)
        is_last = l == depth - 1

        xp, kp, vp, cp = _prompt_mixer(xp, ln1_l, w_in_l, _prompt_bias(rel_table[l]), conv_w[l],
                                       ag_l, cg_l, gmat, w_out_l)
        xp, fp = _prompt_ffn(xp, ln2_l, w_up_l, fconv_w[l], fcb_l, w_down_l, fnorm, is_last)

        xs, ks, vs, us = _sample_mixer(
            xs, ln1_l, w_in_l, cache_attn_k[l].reshape(bs, lc, aw), cache_attn_v[l].reshape(bs, lc, aw),
            _pair_rows(_toeplitz_bias(rel_table[l], ts, lc, lc)), _pair_rows(_toeplitz_bias(rel_table[l], ts, ts, 0)),
            _stream_history(state_mix_conv[l], ts), conv_w[l], ag_l, cg_l, gmat, w_out_l, ts)
        xs, ups = _sample_ffn(xs, ln2_l, w_up_l, _stream_history(state_ffn_conv[l], ts), fconv_w[l], fcb_l,
                              w_down_l, fnorm, ts, is_last)

        new = (kp.reshape(bp, TILE, n_heads, head_dim), vp.reshape(bp, TILE, n_heads, head_dim), cp, fp,
               ks.reshape(bs, ts, n_heads, head_dim), vs.reshape(bs, ts, n_heads, head_dim),
               us.reshape(bs, ts, -1)[:, ts - (CONV_W - 1):], ups.reshape(bs, ts, -1)[:, ts - (CONV_W - 1):])
        for acc, val in zip(outs, new):
            acc.append(val)

    return (xp, xs.reshape(bs, ts, d)) + tuple(jnp.stack(o) for o in outs)
```

```python
import functools

import numpy as np
import jax
import jax.numpy as jnp
from jax import lax
from jax.experimental import pallas as pl
from jax.experimental.pallas import tpu as pltpu

CHUNK = 64
N_PAST_CHUNKS = 8
HEAD_DIM = 64
REL_CLIP = 128
CONV_W = 3
EPS = 1e-6
NEG_INF = -1e30
SCALE = HEAD_DIM ** -0.5

TILE = N_PAST_CHUNKS * CHUNK
BAND = (N_PAST_CHUNKS + 1) * CHUNK
Q_BLOCK = 2 * CHUNK
K_WINDOW = BAND + CHUNK
LANES = 128
HEADS_PER_GROUP = LANES // HEAD_DIM
FF_CHUNK = 256
VMEM_LIMIT = 56 * 1024 * 1024

F32 = jnp.float32
BF16 = jnp.bfloat16


def _resident(shape):
    zeros = (0,) * len(shape)
    return pl.BlockSpec(shape, lambda *_: zeros, pipeline_mode=pl.Buffered(1))


def _rmsnorm(x, g):
    ms = jnp.mean(x * x, axis=-1, keepdims=True)
    return x * lax.rsqrt(ms + EPS) * g


def _group_rmsnorm(x, gmat, g):
    ms = jnp.dot((x * x).astype(BF16), gmat, preferred_element_type=F32)
    return x * lax.rsqrt(ms + EPS) * g


def _shift_rows(u, prev1, prev2):
    row = lax.broadcasted_iota(jnp.int32, u.shape, 0)
    u1 = jnp.where(row == 0, prev1, pltpu.roll(u, 1, 0))
    u2 = jnp.where(row == 0, prev2, jnp.where(row == 1, prev1, pltpu.roll(u, 2, 0)))
    return u1, u2


def _shift_rows_streams(u, hist, period):
    row = lax.broadcasted_iota(jnp.int32, u.shape, 0) % period
    n = u.shape[0]
    u1 = jnp.where(row == 0, pltpu.roll(hist, n - 1, 0), pltpu.roll(u, 1, 0))
    u2 = jnp.where(row < 2, hist, pltpu.roll(u, 2, 0))
    return u1, u2


def _pair_queries(qc):
    lane = lax.broadcasted_iota(jnp.int32, qc.shape, 1)
    zero = jnp.zeros_like(qc)
    return jnp.concatenate([jnp.where(lane < HEAD_DIM, qc, zero),
                            jnp.where(lane >= HEAD_DIM, qc, zero)], axis=0)


def _unpair_output(o2):
    r = o2.shape[0] // 2
    lane = lax.broadcasted_iota(jnp.int32, (r, LANES), 1)
    return jnp.where(lane < HEAD_DIM, o2[:r], o2[r:])


def _qk(qq, kb):
    return lax.dot_general(qq, kb, (((1,), (1,)), ((), ())), preferred_element_type=F32)


def _prompt_mixer_kernel(x_ref, ln1_ref, win_ref, bias_ref, convw_ref, ag_ref, cg_ref, gmat_ref, wout_ref,
                         x1_ref, kout_ref, vout_ref, cst_ref,
                         q_buf, k_buf, v_buf, u_carry, att_buf):
    s = pl.program_id(1)
    last = pl.num_programs(1) - 1
    aw = q_buf.shape[1]
    n_groups = aw // LANES

    @pl.when(s == 0)
    def _():
        k_buf[0:TILE, :] = jnp.zeros((TILE, aw), BF16)
        v_buf[0:TILE, :] = jnp.zeros((TILE, aw), BF16)
        u_carry[...] = jnp.zeros(u_carry.shape, F32)

    x = x_ref[0]
    h = _rmsnorm(x, ln1_ref[...]).astype(BF16)

    def proj(i):
        return jnp.dot(h, win_ref[:, i * aw:(i + 1) * aw], preferred_element_type=F32)

    q_buf[...] = (proj(0) * SCALE).astype(BF16)
    k = proj(1)
    v = proj(2)
    k_buf[TILE:2 * TILE, :] = k.astype(BF16)
    v_buf[TILE:2 * TILE, :] = v.astype(BF16)

    @pl.when(s == last)
    def _():
        kout_ref[0] = k
        vout_ref[0] = v

    lane_w = lax.broadcasted_iota(jnp.int32, (2 * Q_BLOCK, K_WINDOW), 1)

    for i in range(TILE // Q_BLOCK):
        r0 = i * Q_BLOCK
        first_valid = jnp.where(s == 0, TILE - i * Q_BLOCK, 0)
        valid = lane_w >= first_valid
        for p in range(n_groups):
            cols = slice(p * LANES, (p + 1) * LANES)
            qq = _pair_queries(q_buf[pl.ds(r0, Q_BLOCK), cols])
            sc = _qk(qq, k_buf[pl.ds(r0, K_WINDOW), cols]) + bias_ref[p]
            sc = jnp.where(valid, sc, NEG_INF)
            m = jnp.max(sc, axis=-1, keepdims=True)
            e = jnp.exp(sc - m)
            l = jnp.sum(e, axis=-1, keepdims=True)
            o2 = jnp.dot(e.astype(BF16), v_buf[pl.ds(r0, K_WINDOW), cols], preferred_element_type=F32)
            att_buf[pl.ds(r0, Q_BLOCK), cols] = _unpair_output(o2 / l)

    bg = proj(3)
    u = proj(4) * proj(5)
    cst_ref[0] = u[TILE - (CONV_W - 1):, :]
    u1, u2 = _shift_rows(u, u_carry[7:8, :], u_carry[6:7, :])
    z = bg * (convw_ref[0:1, :] * u2 + convw_ref[1:2, :] * u1 + convw_ref[2:3, :] * u)
    u_carry[...] = u[TILE - 8:, :]

    gmat = gmat_ref[...]
    att_n = _group_rmsnorm(att_buf[...], gmat, ag_ref[...]).astype(BF16)
    z_n = _group_rmsnorm(z, gmat, cg_ref[...]).astype(BF16)
    mixed = (jnp.dot(att_n, wout_ref[0:aw, :], preferred_element_type=F32)
             + jnp.dot(z_n, wout_ref[aw:, :], preferred_element_type=F32))
    x1_ref[0] = x + mixed

    k_buf[0:TILE, :] = k_buf[TILE:2 * TILE, :]
    v_buf[0:TILE, :] = v_buf[TILE:2 * TILE, :]


def _prompt_mixer(x, ln1, w_in, bias, conv_w, attn_g, conv_g, gmat, w_out):
    nb, seq, d = x.shape
    aw = attn_g.shape[-1]
    cw = conv_g.shape[-1]
    assert seq % TILE == 0 and aw == cw and aw % LANES == 0
    grid = (nb, seq // TILE)
    return pl.pallas_call(
        _prompt_mixer_kernel,
        grid=grid,
        in_specs=[
            pl.BlockSpec((1, TILE, d), lambda b, s: (b, s, 0)),
            _resident(ln1.shape), _resident(w_in.shape), _resident(bias.shape), _resident(conv_w.shape),
            _resident(attn_g.shape), _resident(conv_g.shape), _resident(gmat.shape), _resident(w_out.shape),
        ],
        out_specs=[
            pl.BlockSpec((1, TILE, d), lambda b, s: (b, s, 0)),
            pl.BlockSpec((1, TILE, aw), lambda b, s: (b, 0, 0)),
            pl.BlockSpec((1, TILE, aw), lambda b, s: (b, 0, 0)),
            pl.BlockSpec((1, CONV_W - 1, cw), lambda b, s: (b, 0, 0)),
        ],
        out_shape=[
            jax.ShapeDtypeStruct((nb, seq, d), F32),
            jax.ShapeDtypeStruct((nb, TILE, aw), F32),
            jax.ShapeDtypeStruct((nb, TILE, aw), F32),
            jax.ShapeDtypeStruct((nb, CONV_W - 1, cw), F32),
        ],
        scratch_shapes=[
            pltpu.VMEM((TILE, aw), BF16),
            pltpu.VMEM((2 * TILE, aw), BF16),
            pltpu.VMEM((2 * TILE, aw), BF16),
            pltpu.VMEM((8, cw), F32),
            pltpu.VMEM((TILE, aw), F32),
        ],
        compiler_params=pltpu.CompilerParams(
            dimension_semantics=("arbitrary", "arbitrary"), vmem_limit_bytes=VMEM_LIMIT),
        name="prompt_mixer",
    )(x, ln1, w_in, bias, conv_w, attn_g, conv_g, gmat, w_out)


def _ffn_chunks(h, x, wup_ref, fcw_ref, fcb_ref, wdn_ref, shift, store_state):
    d_ff = wdn_ref.shape[0]
    acc = x
    for j in range(d_ff // FF_CHUNK):
        halves = []
        for c0 in (j * FF_CHUNK, d_ff + j * FF_CHUNK):
            cols = slice(c0, c0 + FF_CHUNK)
            up = jnp.dot(h, wup_ref[:, cols], preferred_element_type=F32)
            u1, u2 = shift(up, cols)
            store_state(up, cols)
            halves.append(fcw_ref[0:1, cols] * u2 + fcw_ref[1:2, cols] * u1 + fcw_ref[2:3, cols] * up
                          + fcb_ref[:, cols])
        a_g, a_v = halves
        act = (a_g * jax.nn.sigmoid(a_g) * a_v).astype(BF16)
        acc = acc + jnp.dot(act, wdn_ref[j * FF_CHUNK:(j + 1) * FF_CHUNK, :], preferred_element_type=F32)
    return acc


def _prompt_ffn_kernel(x_ref, ln2_ref, wup_ref, fcw_ref, fcb_ref, wdn_ref, fn_ref,
                       y_ref, fst_ref, up_carry, *, final_norm):
    s = pl.program_id(1)

    @pl.when(s == 0)
    def _():
        up_carry[...] = jnp.zeros(up_carry.shape, F32)

    x = x_ref[0]
    h = _rmsnorm(x, ln2_ref[...]).astype(BF16)

    def shift(up, cols):
        return _shift_rows(up, up_carry[7:8, cols], up_carry[6:7, cols])

    def store_state(up, cols):
        fst_ref[0, :, cols] = up[TILE - (CONV_W - 1):, :]
        up_carry[:, cols] = up[TILE - 8:, :]

    y = _ffn_chunks(h, x, wup_ref, fcw_ref, fcb_ref, wdn_ref, shift, store_state)
    if final_norm:
        y = _rmsnorm(y, fn_ref[...])
    y_ref[0] = y


def _prompt_ffn(x, ln2, w_up, fconv_w, fconv_b, w_down, fnorm, final_norm):
    nb, seq, d = x.shape
    f2 = w_up.shape[1]
    assert seq % TILE == 0 and (f2 // 2) % FF_CHUNK == 0
    return pl.pallas_call(
        functools.partial(_prompt_ffn_kernel, final_norm=final_norm),
        grid=(nb, seq // TILE),
        in_specs=[
            pl.BlockSpec((1, TILE, d), lambda b, s: (b, s, 0)),
            _resident(ln2.shape), _resident(w_up.shape), _resident(fconv_w.shape), _resident(fconv_b.shape),
            _resident(w_down.shape), _resident(fnorm.shape),
        ],
        out_specs=[
            pl.BlockSpec((1, TILE, d), lambda b, s: (b, s, 0)),
            pl.BlockSpec((1, CONV_W - 1, f2), lambda b, s: (b, 0, 0)),
        ],
        out_shape=[
            jax.ShapeDtypeStruct((nb, seq, d), F32),
            jax.ShapeDtypeStruct((nb, CONV_W - 1, f2), F32),
        ],
        scratch_shapes=[pltpu.VMEM((8, f2), F32)],
        compiler_params=pltpu.CompilerParams(
            dimension_semantics=("arbitrary", "arbitrary"), vmem_limit_bytes=VMEM_LIMIT),
        name="prompt_ffn",
    )(x, ln2, w_up, fconv_w, fconv_b, w_down, fnorm)


def _sample_mixer_kernel(x_ref, ln1_ref, win_ref, ck_ref, cv_ref, bias_c_ref, bias_n_ref, hist_ref, convw_ref,
                         ag_ref, cg_ref, gmat_ref, wout_ref,
                         x1_ref, kout_ref, vout_ref, cst_ref,
                         q_buf, k_buf, v_buf, att_buf, z_buf, *, t_new):
    b = pl.program_id(0)
    last = pl.num_programs(0) - 1
    aw = q_buf.shape[1]
    n_groups = aw // LANES

    @pl.when(b == 0)
    def _():
        h = _rmsnorm(x_ref[...], ln1_ref[...]).astype(BF16)

        def proj(i):
            return jnp.dot(h, win_ref[:, i * aw:(i + 1) * aw], preferred_element_type=F32)

        q_buf[...] = (proj(0) * SCALE).astype(BF16)
        k = proj(1)
        v = proj(2)
        kout_ref[...] = k
        vout_ref[...] = v
        k_buf[...] = k.astype(BF16)
        v_buf[...] = v.astype(BF16)
        bg = proj(3)
        u = proj(4) * proj(5)
        cst_ref[...] = u
        u1, u2 = _shift_rows_streams(u, hist_ref[...], t_new)
        z_buf[...] = bg * (convw_ref[0:1, :] * u2 + convw_ref[1:2, :] * u1 + convw_ref[2:3, :] * u)

    r0 = pl.multiple_of(b * t_new, t_new)
    ck = ck_ref[0].astype(BF16)
    cv = cv_ref[0].astype(BF16)
    for p in range(n_groups):
        cols = slice(p * LANES, (p + 1) * LANES)
        qq = _pair_queries(q_buf[pl.ds(r0, t_new), cols])
        sc_c = _qk(qq, ck[:, cols]) + bias_c_ref[p]
        sc_n = _qk(qq, k_buf[pl.ds(r0, t_new), cols]) + bias_n_ref[p]
        m = jnp.maximum(jnp.max(sc_c, axis=-1, keepdims=True), jnp.max(sc_n, axis=-1, keepdims=True))
        e_c = jnp.exp(sc_c - m)
        e_n = jnp.exp(sc_n - m)
        l = jnp.sum(e_c, axis=-1, keepdims=True) + jnp.sum(e_n, axis=-1, keepdims=True)
        o2 = (jnp.dot(e_c.astype(BF16), cv[:, cols], preferred_element_type=F32)
              + jnp.dot(e_n.astype(BF16), v_buf[pl.ds(r0, t_new), cols], preferred_element_type=F32))
        att_buf[pl.ds(r0, t_new), cols] = _unpair_output(o2 / l)

    @pl.when(b == last)
    def _():
        gmat = gmat_ref[...]
        att_n = _group_rmsnorm(att_buf[...], gmat, ag_ref[...]).astype(BF16)
        z_n = _group_rmsnorm(z_buf[...], gmat, cg_ref[...]).astype(BF16)
        mixed = (jnp.dot(att_n, wout_ref[0:aw, :], preferred_element_type=F32)
                 + jnp.dot(z_n, wout_ref[aw:, :], preferred_element_type=F32))
        x1_ref[...] = x_ref[...] + mixed


def _sample_mixer(x, ln1, w_in, ck, cv, layer, bias_c, bias_n, hist, conv_w, attn_g, conv_g, gmat, w_out, t_new):
    n, d = x.shape
    nb = n // t_new
    _, lc, aw = ck.shape
    assert n == nb * t_new and ck.shape[0] % nb == 0
    full = lambda shape: pl.BlockSpec(shape, lambda b: (0,) * len(shape))
    return pl.pallas_call(
        functools.partial(_sample_mixer_kernel, t_new=t_new),
        grid=(nb,),
        in_specs=[
            _resident(x.shape), _resident(ln1.shape), _resident(w_in.shape),
            pl.BlockSpec((1, lc, aw), lambda b: (layer * nb + b, 0, 0)),
            pl.BlockSpec((1, lc, aw), lambda b: (layer * nb + b, 0, 0)),
            _resident(bias_c.shape), _resident(bias_n.shape), _resident(hist.shape), _resident(conv_w.shape),
            _resident(attn_g.shape), _resident(conv_g.shape), _resident(gmat.shape), _resident(w_out.shape),
        ],
        out_specs=[full((n, d)), full((n, aw)), full((n, aw)), full((n, aw))],
        out_shape=[
            jax.ShapeDtypeStruct((n, d), F32),
            jax.ShapeDtypeStruct((n, aw), F32),
            jax.ShapeDtypeStruct((n, aw), F32),
            jax.ShapeDtypeStruct((n, aw), F32),
        ],
        scratch_shapes=[
            pltpu.VMEM((n, aw), BF16), pltpu.VMEM((n, aw), BF16), pltpu.VMEM((n, aw), BF16),
            pltpu.VMEM((n, aw), F32), pltpu.VMEM((n, aw), F32),
        ],
        compiler_params=pltpu.CompilerParams(
            dimension_semantics=("arbitrary",), vmem_limit_bytes=VMEM_LIMIT),
        name="sample_mixer",
    )(x, ln1, w_in, ck, cv, bias_c, bias_n, hist, conv_w, attn_g, conv_g, gmat, w_out)


def _sample_ffn_kernel(x_ref, ln2_ref, wup_ref, hist_ref, fcw_ref, fcb_ref, wdn_ref, fn_ref,
                       y_ref, up_ref, *, t_new, final_norm):
    x = x_ref[...]
    h = _rmsnorm(x, ln2_ref[...]).astype(BF16)

    def shift(up, cols):
        return _shift_rows_streams(up, hist_ref[:, cols], t_new)

    def store_state(up, cols):
        up_ref[:, cols] = up

    y = _ffn_chunks(h, x, wup_ref, fcw_ref, fcb_ref, wdn_ref, shift, store_state)
    if final_norm:
        y = _rmsnorm(y, fn_ref[...])
    y_ref[...] = y


def _sample_ffn(x, ln2, w_up, hist, fconv_w, fconv_b, w_down, fnorm, t_new, final_norm):
    n, d = x.shape
    f2 = w_up.shape[1]
    args = (x, ln2, w_up, hist, fconv_w, fconv_b, w_down, fnorm)
    full = lambda shape: pl.BlockSpec(shape, lambda i: (0,) * len(shape))
    return pl.pallas_call(
        functools.partial(_sample_ffn_kernel, t_new=t_new, final_norm=final_norm),
        grid=(1,),
        in_specs=[_resident(a.shape) for a in args],
        out_specs=[full((n, d)), full((n, f2))],
        out_shape=[jax.ShapeDtypeStruct((n, d), F32), jax.ShapeDtypeStruct((n, f2), F32)],
        compiler_params=pltpu.CompilerParams(
            dimension_semantics=("arbitrary",), vmem_limit_bytes=VMEM_LIMIT),
        name="sample_ffn",
    )(*args)


def _toeplitz_bias(table, n_q, n_k, offset):
    h = table.shape[0]
    p = n_q + n_k
    m = np.minimum(np.arange(p), p - 2)
    diag = table[:, np.clip(offset + n_q - 1 - m, -REL_CLIP, REL_CLIP) + REL_CLIP]
    skew = jnp.tile(diag, (1, n_q))[:, :n_q * (p - 1)].reshape(h, n_q, p - 1)
    return skew[:, :, n_q - 1:n_q - 1 + n_k]


def _pair_rows(bias):
    h, q, k = bias.shape
    return bias.reshape(h // HEADS_PER_GROUP, HEADS_PER_GROUP * q, k)


def _prompt_bias(table):
    q = np.arange(Q_BLOCK)[:, None]
    k = np.arange(K_WINDOW)[None, :]
    in_band = np.where(q < CHUNK, k < BAND, k >= CHUNK)
    bias = _toeplitz_bias(table, Q_BLOCK, K_WINDOW, N_PAST_CHUNKS * CHUNK)
    return _pair_rows(jnp.where(in_band[None], bias, NEG_INF))


def _group_mean_matrix(width, group):
    g = np.arange(width) // group
    return jnp.asarray((g[:, None] == g[None, :]).astype(np.float32) / group, dtype=BF16)


def _stream_history(state, t_new):
    nb, hw, c = state.shape
    return jnp.pad(state, ((0, 0), (0, t_new - hw), (0, 0))).reshape(nb * t_new, c)


def kernel(x_prompt, x_sample, cache_attn_k, cache_attn_v, state_mix_conv, state_ffn_conv, ln1, w_in, rel_table,
           conv_w, attn_g, conv_g, w_out, ln2, w_up, fconv_w, fconv_b, w_down, final_norm):
    depth = ln1.shape[0]
    bp, tp, d = x_prompt.shape
    bs, ts, _ = x_sample.shape
    lc = cache_attn_k.shape[2]
    n_heads, head_dim = cache_attn_k.shape[3:]
    aw = n_heads * head_dim
    assert head_dim == HEAD_DIM and tp >= TILE and lc == TILE and ts >= CONV_W - 1

    gmat = _group_mean_matrix(aw, HEAD_DIM)
    fnorm = final_norm.reshape(1, d)
    ck_all = cache_attn_k.reshape(depth * bs, lc, aw)
    cv_all = cache_attn_v.reshape(depth * bs, lc, aw)

    xp = x_prompt
    xs = x_sample.reshape(bs * ts, d)
    outs = [[] for _ in range(8)]
    for l in range(depth):
        w_in_l = w_in[l].astype(BF16)
        w_out_l = w_out[l].astype(BF16)
        w_up_l = w_up[l].astype(BF16)
        w_down_l = w_down[l].astype(BF16)
        ln1_l, ln2_l = ln1[l].reshape(1, d), ln2[l].reshape(1, d)
        ag_l, cg_l = attn_g[l].reshape(1, -1), conv_g[l].reshape(1, -1)
        fcb_l = fconv_b[l].reshape(1, -1)
        is_last = l == depth - 1

        xp, kp, vp, cp = _prompt_mixer(xp, ln1_l, w_in_l, _prompt_bias(rel_table[l]), conv_w[l],
                                       ag_l, cg_l, gmat, w_out_l)
        xp, fp = _prompt_ffn(xp, ln2_l, w_up_l, fconv_w[l], fcb_l, w_down_l, fnorm, is_last)

        xs, ks, vs, us = _sample_mixer(
            xs, ln1_l, w_in_l, ck_all, cv_all, l,
            _pair_rows(_toeplitz_bias(rel_table[l], ts, lc, lc)), _pair_rows(_toeplitz_bias(rel_table[l], ts, ts, 0)),
            _stream_history(state_mix_conv[l], ts), conv_w[l], ag_l, cg_l, gmat, w_out_l, ts)
        xs, ups = _sample_ffn(xs, ln2_l, w_up_l, _stream_history(state_ffn_conv[l], ts), fconv_w[l], fcb_l,
                              w_down_l, fnorm, ts, is_last)

        new = (kp.reshape(bp, TILE, n_heads, head_dim), vp.reshape(bp, TILE, n_heads, head_dim), cp, fp,
               ks.reshape(bs, ts, n_heads, head_dim), vs.reshape(bs, ts, n_heads, head_dim),
               us.reshape(bs, ts, -1)[:, ts - (CONV_W - 1):], ups.reshape(bs, ts, -1)[:, ts - (CONV_W - 1):])
        for acc, val in zip(outs, new):
            acc.append(val)

    return (xp, xs.reshape(bs, ts, d)) + tuple(jnp.stack(o) for o in outs)
```

```python
import functools

import numpy as np
import jax
import jax.numpy as jnp
from jax import lax
from jax.experimental import pallas as pl
from jax.experimental.pallas import tpu as pltpu

CHUNK = 64
N_PAST_CHUNKS = 8
HEAD_DIM = 64
REL_CLIP = 128
CONV_W = 3
EPS = 1e-6
NEG_INF = -1e30
SCALE = HEAD_DIM ** -0.5

TILE = N_PAST_CHUNKS * CHUNK
BAND = (N_PAST_CHUNKS + 1) * CHUNK
Q_BLOCK = 2 * CHUNK
K_WINDOW = BAND + CHUNK
LANES = 128
HEADS_PER_GROUP = LANES // HEAD_DIM
FF_CHUNK = 256
VMEM_LIMIT = 56 * 1024 * 1024

F32 = jnp.float32
BF16 = jnp.bfloat16


def _resident(shape):
    zeros = (0,) * len(shape)
    return pl.BlockSpec(shape, lambda *_: zeros, pipeline_mode=pl.Buffered(1))


def _rmsnorm(x, g):
    ms = jnp.mean(x * x, axis=-1, keepdims=True)
    return x * lax.rsqrt(ms + EPS) * g


def _group_rmsnorm(x, gmat, g):
    ms = jnp.dot((x * x).astype(BF16), gmat, preferred_element_type=F32)
    return x * lax.rsqrt(ms + EPS) * g


def _shift_rows(u, prev1, prev2):
    row = lax.broadcasted_iota(jnp.int32, u.shape, 0)
    u1 = jnp.where(row == 0, prev1, pltpu.roll(u, 1, 0))
    u2 = jnp.where(row == 0, prev2, jnp.where(row == 1, prev1, pltpu.roll(u, 2, 0)))
    return u1, u2


def _shift_rows_streams(u, hist, period):
    row = lax.broadcasted_iota(jnp.int32, u.shape, 0) % period
    n = u.shape[0]
    u1 = jnp.where(row == 0, pltpu.roll(hist, n - 1, 0), pltpu.roll(u, 1, 0))
    u2 = jnp.where(row < 2, hist, pltpu.roll(u, 2, 0))
    return u1, u2


def _pair_queries(qc):
    lane = lax.broadcasted_iota(jnp.int32, qc.shape, 1)
    zero = jnp.zeros_like(qc)
    return jnp.concatenate([jnp.where(lane < HEAD_DIM, qc, zero),
                            jnp.where(lane >= HEAD_DIM, qc, zero)], axis=0)


def _unpair_output(o2):
    r = o2.shape[0] // 2
    lane = lax.broadcasted_iota(jnp.int32, (r, LANES), 1)
    return jnp.where(lane < HEAD_DIM, o2[:r], o2[r:])


def _qk(qq, kb):
    return lax.dot_general(qq, kb, (((1,), (1,)), ((), ())), preferred_element_type=F32)


def _prompt_mixer_kernel(x_ref, ln1_ref, win_ref, bias_ref, convw_ref, ag_ref, cg_ref, gmat_ref, wout_ref,
                         x1_ref, kout_ref, vout_ref, cst_ref,
                         q_buf, k_buf, v_buf, u_carry, att_buf):
    s = pl.program_id(1)
    last = pl.num_programs(1) - 1
    aw = q_buf.shape[1]
    n_groups = aw // LANES

    @pl.when(s == 0)
    def _():
        k_buf[0:TILE, :] = jnp.zeros((TILE, aw), BF16)
        v_buf[0:TILE, :] = jnp.zeros((TILE, aw), BF16)
        u_carry[...] = jnp.zeros(u_carry.shape, F32)

    x = x_ref[0]
    h = _rmsnorm(x, ln1_ref[...]).astype(BF16)

    def proj(i):
        return jnp.dot(h, win_ref[:, i * aw:(i + 1) * aw], preferred_element_type=F32)

    q_buf[...] = (proj(0) * SCALE).astype(BF16)
    k = proj(1)
    v = proj(2)
    k_buf[TILE:2 * TILE, :] = k.astype(BF16)
    v_buf[TILE:2 * TILE, :] = v.astype(BF16)

    @pl.when(s == last)
    def _():
        kout_ref[0] = k
        vout_ref[0] = v

    lane_w = lax.broadcasted_iota(jnp.int32, (2 * Q_BLOCK, K_WINDOW), 1)

    for i in range(TILE // Q_BLOCK):
        r0 = i * Q_BLOCK
        first_valid = jnp.where(s == 0, TILE - i * Q_BLOCK, 0)
        valid = lane_w >= first_valid
        for p in range(n_groups):
            cols = slice(p * LANES, (p + 1) * LANES)
            qq = _pair_queries(q_buf[pl.ds(r0, Q_BLOCK), cols])
            sc = _qk(qq, k_buf[pl.ds(r0, K_WINDOW), cols]) + bias_ref[p]
            sc = jnp.where(valid, sc, NEG_INF)
            m = jnp.max(sc, axis=-1, keepdims=True)
            e = jnp.exp(sc - m)
            l = jnp.sum(e, axis=-1, keepdims=True)
            o2 = jnp.dot(e.astype(BF16), v_buf[pl.ds(r0, K_WINDOW), cols], preferred_element_type=F32)
            att_buf[pl.ds(r0, Q_BLOCK), cols] = _unpair_output(o2 / l)

    bg = proj(3)
    u = proj(4) * proj(5)
    cst_ref[0] = u[TILE - (CONV_W - 1):, :]
    u1, u2 = _shift_rows(u, u_carry[7:8, :], u_carry[6:7, :])
    z = bg * (convw_ref[0:1, :] * u2 + convw_ref[1:2, :] * u1 + convw_ref[2:3, :] * u)
    u_carry[...] = u[TILE - 8:, :]

    gmat = gmat_ref[...]
    att_n = _group_rmsnorm(att_buf[...], gmat, ag_ref[...]).astype(BF16)
    z_n = _group_rmsnorm(z, gmat, cg_ref[...]).astype(BF16)
    mixed = (jnp.dot(att_n, wout_ref[0:aw, :], preferred_element_type=F32)
             + jnp.dot(z_n, wout_ref[aw:, :], preferred_element_type=F32))
    x1_ref[0] = x + mixed

    k_buf[0:TILE, :] = k_buf[TILE:2 * TILE, :]
    v_buf[0:TILE, :] = v_buf[TILE:2 * TILE, :]


def _prompt_mixer(x, ln1, w_in, bias, conv_w, attn_g, conv_g, gmat, w_out):
    nb, seq, d = x.shape
    aw = attn_g.shape[-1]
    cw = conv_g.shape[-1]
    assert seq % TILE == 0 and aw == cw and aw % LANES == 0
    grid = (nb, seq // TILE)
    return pl.pallas_call(
        _prompt_mixer_kernel,
        grid=grid,
        in_specs=[
            pl.BlockSpec((1, TILE, d), lambda b, s: (b, s, 0)),
            _resident(ln1.shape), _resident(w_in.shape), _resident(bias.shape), _resident(conv_w.shape),
            _resident(attn_g.shape), _resident(conv_g.shape), _resident(gmat.shape), _resident(w_out.shape),
        ],
        out_specs=[
            pl.BlockSpec((1, TILE, d), lambda b, s: (b, s, 0)),
            pl.BlockSpec((1, TILE, aw), lambda b, s: (b, 0, 0)),
            pl.BlockSpec((1, TILE, aw), lambda b, s: (b, 0, 0)),
            pl.BlockSpec((1, CONV_W - 1, cw), lambda b, s: (b, 0, 0)),
        ],
        out_shape=[
            jax.ShapeDtypeStruct((nb, seq, d), F32),
            jax.ShapeDtypeStruct((nb, TILE, aw), F32),
            jax.ShapeDtypeStruct((nb, TILE, aw), F32),
            jax.ShapeDtypeStruct((nb, CONV_W - 1, cw), F32),
        ],
        scratch_shapes=[
            pltpu.VMEM((TILE, aw), BF16),
            pltpu.VMEM((2 * TILE, aw), BF16),
            pltpu.VMEM((2 * TILE, aw), BF16),
            pltpu.VMEM((8, cw), F32),
            pltpu.VMEM((TILE, aw), F32),
        ],
        compiler_params=pltpu.CompilerParams(
            dimension_semantics=("arbitrary", "arbitrary"), vmem_limit_bytes=VMEM_LIMIT),
        name="prompt_mixer",
    )(x, ln1, w_in, bias, conv_w, attn_g, conv_g, gmat, w_out)


def _ffn_chunks(h, x, wup_ref, fcw_ref, fcb_ref, wdn_ref, shift, store_state):
    d_ff = wdn_ref.shape[0]
    acts = []
    for j in range(d_ff // FF_CHUNK):
        halves = []
        for c0 in (j * FF_CHUNK, d_ff + j * FF_CHUNK):
            cols = slice(c0, c0 + FF_CHUNK)
            up = jnp.dot(h, wup_ref[:, cols], preferred_element_type=F32)
            u1, u2 = shift(up, cols)
            store_state(up, cols)
            halves.append(fcw_ref[0:1, cols] * u2 + fcw_ref[1:2, cols] * u1 + fcw_ref[2:3, cols] * up
                          + fcb_ref[:, cols])
        a_g, a_v = halves
        acts.append((a_g * jax.nn.sigmoid(a_g) * a_v).astype(BF16))
    act = jnp.concatenate(acts, axis=1)
    return x + jnp.dot(act, wdn_ref[...], preferred_element_type=F32)


def _prompt_ffn_kernel(x_ref, ln2_ref, wup_ref, fcw_ref, fcb_ref, wdn_ref, fn_ref,
                       y_ref, fst_ref, up_carry, *, final_norm):
    s = pl.program_id(1)

    @pl.when(s == 0)
    def _():
        up_carry[...] = jnp.zeros(up_carry.shape, F32)

    x = x_ref[0]
    h = _rmsnorm(x, ln2_ref[...]).astype(BF16)

    def shift(up, cols):
        return _shift_rows(up, up_carry[7:8, cols], up_carry[6:7, cols])

    def store_state(up, cols):
        fst_ref[0, :, cols] = up[TILE - (CONV_W - 1):, :]
        up_carry[:, cols] = up[TILE - 8:, :]

    y = _ffn_chunks(h, x, wup_ref, fcw_ref, fcb_ref, wdn_ref, shift, store_state)
    if final_norm:
        y = _rmsnorm(y, fn_ref[...])
    y_ref[0] = y


def _prompt_ffn(x, ln2, w_up, fconv_w, fconv_b, w_down, fnorm, final_norm):
    nb, seq, d = x.shape
    f2 = w_up.shape[1]
    assert seq % TILE == 0 and (f2 // 2) % FF_CHUNK == 0
    return pl.pallas_call(
        functools.partial(_prompt_ffn_kernel, final_norm=final_norm),
        grid=(nb, seq // TILE),
        in_specs=[
            pl.BlockSpec((1, TILE, d), lambda b, s: (b, s, 0)),
            _resident(ln2.shape), _resident(w_up.shape), _resident(fconv_w.shape), _resident(fconv_b.shape),
            _resident(w_down.shape), _resident(fnorm.shape),
        ],
        out_specs=[
            pl.BlockSpec((1, TILE, d), lambda b, s: (b, s, 0)),
            pl.BlockSpec((1, CONV_W - 1, f2), lambda b, s: (b, 0, 0)),
        ],
        out_shape=[
            jax.ShapeDtypeStruct((nb, seq, d), F32),
            jax.ShapeDtypeStruct((nb, CONV_W - 1, f2), F32),
        ],
        scratch_shapes=[pltpu.VMEM((8, f2), F32)],
        compiler_params=pltpu.CompilerParams(
            dimension_semantics=("arbitrary", "arbitrary"), vmem_limit_bytes=VMEM_LIMIT),
        name="prompt_ffn",
    )(x, ln2, w_up, fconv_w, fconv_b, w_down, fnorm)


def _sample_mixer_kernel(x_ref, ln1_ref, win_ref, ck_ref, cv_ref, bias_c_ref, bias_n_ref, hist_ref, convw_ref,
                         ag_ref, cg_ref, gmat_ref, wout_ref,
                         x1_ref, kout_ref, vout_ref, cst_ref,
                         q_buf, k_buf, v_buf, att_buf, z_buf, *, t_new):
    b = pl.program_id(0)
    last = pl.num_programs(0) - 1
    aw = q_buf.shape[1]
    n_groups = aw // LANES

    @pl.when(b == 0)
    def _():
        h = _rmsnorm(x_ref[...], ln1_ref[...]).astype(BF16)

        def proj(i):
            return jnp.dot(h, win_ref[:, i * aw:(i + 1) * aw], preferred_element_type=F32)

        q_buf[...] = (proj(0) * SCALE).astype(BF16)
        k = proj(1)
        v = proj(2)
        kout_ref[...] = k
        vout_ref[...] = v
        k_buf[...] = k.astype(BF16)
        v_buf[...] = v.astype(BF16)
        bg = proj(3)
        u = proj(4) * proj(5)
        cst_ref[...] = u
        u1, u2 = _shift_rows_streams(u, hist_ref[...], t_new)
        z_buf[...] = bg * (convw_ref[0:1, :] * u2 + convw_ref[1:2, :] * u1 + convw_ref[2:3, :] * u)

    r0 = pl.multiple_of(b * t_new, t_new)
    ck = ck_ref[0].astype(BF16)
    cv = cv_ref[0].astype(BF16)
    for p in range(n_groups):
        cols = slice(p * LANES, (p + 1) * LANES)
        qq = _pair_queries(q_buf[pl.ds(r0, t_new), cols])
        sc_c = _qk(qq, ck[:, cols]) + bias_c_ref[p]
        sc_n = _qk(qq, k_buf[pl.ds(r0, t_new), cols]) + bias_n_ref[p]
        m = jnp.maximum(jnp.max(sc_c, axis=-1, keepdims=True), jnp.max(sc_n, axis=-1, keepdims=True))
        e_c = jnp.exp(sc_c - m)
        e_n = jnp.exp(sc_n - m)
        l = jnp.sum(e_c, axis=-1, keepdims=True) + jnp.sum(e_n, axis=-1, keepdims=True)
        o2 = (jnp.dot(e_c.astype(BF16), cv[:, cols], preferred_element_type=F32)
              + jnp.dot(e_n.astype(BF16), v_buf[pl.ds(r0, t_new), cols], preferred_element_type=F32))
        att_buf[pl.ds(r0, t_new), cols] = _unpair_output(o2 / l)

    @pl.when(b == last)
    def _():
        gmat = gmat_ref[...]
        att_n = _group_rmsnorm(att_buf[...], gmat, ag_ref[...]).astype(BF16)
        z_n = _group_rmsnorm(z_buf[...], gmat, cg_ref[...]).astype(BF16)
        mixed = (jnp.dot(att_n, wout_ref[0:aw, :], preferred_element_type=F32)
                 + jnp.dot(z_n, wout_ref[aw:, :], preferred_element_type=F32))
        x1_ref[...] = x_ref[...] + mixed


def _sample_mixer(x, ln1, w_in, ck, cv, layer, bias_c, bias_n, hist, conv_w, attn_g, conv_g, gmat, w_out, t_new):
    n, d = x.shape
    nb = n // t_new
    _, lc, aw = ck.shape
    assert n == nb * t_new and ck.shape[0] % nb == 0
    full = lambda shape: pl.BlockSpec(shape, lambda b: (0,) * len(shape))
    return pl.pallas_call(
        functools.partial(_sample_mixer_kernel, t_new=t_new),
        grid=(nb,),
        in_specs=[
            _resident(x.shape), _resident(ln1.shape), _resident(w_in.shape),
            pl.BlockSpec((1, lc, aw), lambda b: (layer * nb + b, 0, 0)),
            pl.BlockSpec((1, lc, aw), lambda b: (layer * nb + b, 0, 0)),
            _resident(bias_c.shape), _resident(bias_n.shape), _resident(hist.shape), _resident(conv_w.shape),
            _resident(attn_g.shape), _resident(conv_g.shape), _resident(gmat.shape), _resident(w_out.shape),
        ],
        out_specs=[full((n, d)), full((n, aw)), full((n, aw)), full((n, aw))],
        out_shape=[
            jax.ShapeDtypeStruct((n, d), F32),
            jax.ShapeDtypeStruct((n, aw), F32),
            jax.ShapeDtypeStruct((n, aw), F32),
            jax.ShapeDtypeStruct((n, aw), F32),
        ],
        scratch_shapes=[
            pltpu.VMEM((n, aw), BF16), pltpu.VMEM((n, aw), BF16), pltpu.VMEM((n, aw), BF16),
            pltpu.VMEM((n, aw), F32), pltpu.VMEM((n, aw), F32),
        ],
        compiler_params=pltpu.CompilerParams(
            dimension_semantics=("arbitrary",), vmem_limit_bytes=VMEM_LIMIT),
        name="sample_mixer",
    )(x, ln1, w_in, ck, cv, bias_c, bias_n, hist, conv_w, attn_g, conv_g, gmat, w_out)


def _sample_ffn_kernel(x_ref, ln2_ref, wup_ref, hist_ref, fcw_ref, fcb_ref, wdn_ref, fn_ref,
                       y_ref, up_ref, *, t_new, final_norm):
    x = x_ref[...]
    h = _rmsnorm(x, ln2_ref[...]).astype(BF16)

    def shift(up, cols):
        return _shift_rows_streams(up, hist_ref[:, cols], t_new)

    def store_state(up, cols):
        up_ref[:, cols] = up

    y = _ffn_chunks(h, x, wup_ref, fcw_ref, fcb_ref, wdn_ref, shift, store_state)
    if final_norm:
        y = _rmsnorm(y, fn_ref[...])
    y_ref[...] = y


def _sample_ffn(x, ln2, w_up, hist, fconv_w, fconv_b, w_down, fnorm, t_new, final_norm):
    n, d = x.shape
    f2 = w_up.shape[1]
    args = (x, ln2, w_up, hist, fconv_w, fconv_b, w_down, fnorm)
    full = lambda shape: pl.BlockSpec(shape, lambda i: (0,) * len(shape))
    return pl.pallas_call(
        functools.partial(_sample_ffn_kernel, t_new=t_new, final_norm=final_norm),
        grid=(1,),
        in_specs=[_resident(a.shape) for a in args],
        out_specs=[full((n, d)), full((n, f2))],
        out_shape=[jax.ShapeDtypeStruct((n, d), F32), jax.ShapeDtypeStruct((n, f2), F32)],
        compiler_params=pltpu.CompilerParams(
            dimension_semantics=("arbitrary",), vmem_limit_bytes=VMEM_LIMIT),
        name="sample_ffn",
    )(*args)


def _toeplitz_bias(table, n_q, n_k, offset):
    h = table.shape[0]
    p = n_q + n_k
    m = np.minimum(np.arange(p), p - 2)
    diag = table[:, np.clip(offset + n_q - 1 - m, -REL_CLIP, REL_CLIP) + REL_CLIP]
    skew = jnp.tile(diag, (1, n_q))[:, :n_q * (p - 1)].reshape(h, n_q, p - 1)
    return skew[:, :, n_q - 1:n_q - 1 + n_k]


def _pair_rows(bias):
    h, q, k = bias.shape
    return bias.reshape(h // HEADS_PER_GROUP, HEADS_PER_GROUP * q, k)


def _prompt_bias(table):
    q = np.arange(Q_BLOCK)[:, None]
    k = np.arange(K_WINDOW)[None, :]
    in_band = np.where(q < CHUNK, k < BAND, k >= CHUNK)
    bias = _toeplitz_bias(table, Q_BLOCK, K_WINDOW, N_PAST_CHUNKS * CHUNK)
    return _pair_rows(jnp.where(in_band[None], bias, NEG_INF))


def _group_mean_matrix(width, group):
    g = np.arange(width) // group
    return jnp.asarray((g[:, None] == g[None, :]).astype(np.float32) / group, dtype=BF16)


def _stream_history(state, t_new):
    nb, hw, c = state.shape
    return jnp.pad(state, ((0, 0), (0, t_new - hw), (0, 0))).reshape(nb * t_new, c)


def kernel(x_prompt, x_sample, cache_attn_k, cache_attn_v, state_mix_conv, state_ffn_conv, ln1, w_in, rel_table,
           conv_w, attn_g, conv_g, w_out, ln2, w_up, fconv_w, fconv_b, w_down, final_norm):
    depth = ln1.shape[0]
    bp, tp, d = x_prompt.shape
    bs, ts, _ = x_sample.shape
    lc = cache_attn_k.shape[2]
    n_heads, head_dim = cache_attn_k.shape[3:]
    aw = n_heads * head_dim
    assert head_dim == HEAD_DIM and tp >= TILE and lc == TILE and ts >= CONV_W - 1

    gmat = _group_mean_matrix(aw, HEAD_DIM)
    fnorm = final_norm.reshape(1, d)
    ck_all = cache_attn_k.reshape(depth * bs, lc, aw)
    cv_all = cache_attn_v.reshape(depth * bs, lc, aw)

    xp = x_prompt
    xs = x_sample.reshape(bs * ts, d)
    outs = [[] for _ in range(8)]
    for l in range(depth):
        w_in_l = w_in[l].astype(BF16)
        w_out_l = w_out[l].astype(BF16)
        w_up_l = w_up[l].astype(BF16)
        w_down_l = w_down[l].astype(BF16)
        ln1_l, ln2_l = ln1[l].reshape(1, d), ln2[l].reshape(1, d)
        ag_l, cg_l = attn_g[l].reshape(1, -1), conv_g[l].reshape(1, -1)
        fcb_l = fconv_b[l].reshape(1, -1)
        is_last = l == depth - 1

        xp, kp, vp, cp = _prompt_mixer(xp, ln1_l, w_in_l, _prompt_bias(rel_table[l]), conv_w[l],
                                       ag_l, cg_l, gmat, w_out_l)
        xp, fp = _prompt_ffn(xp, ln2_l, w_up_l, fconv_w[l], fcb_l, w_down_l, fnorm, is_last)

        xs, ks, vs, us = _sample_mixer(
            xs, ln1_l, w_in_l, ck_all, cv_all, l,
            _pair_rows(_toeplitz_bias(rel_table[l], ts, lc, lc)), _pair_rows(_toeplitz_bias(rel_table[l], ts, ts, 0)),
            _stream_history(state_mix_conv[l], ts), conv_w[l], ag_l, cg_l, gmat, w_out_l, ts)
        xs, ups = _sample_ffn(xs, ln2_l, w_up_l, _stream_history(state_ffn_conv[l], ts), fconv_w[l], fcb_l,
                              w_down_l, fnorm, ts, is_last)

        new = (kp.reshape(bp, TILE, n_heads, head_dim), vp.reshape(bp, TILE, n_heads, head_dim), cp, fp,
               ks.reshape(bs, ts, n_heads, head_dim), vs.reshape(bs, ts, n_heads, head_dim),
               us.reshape(bs, ts, -1)[:, ts - (CONV_W - 1):], ups.reshape(bs, ts, -1)[:, ts - (CONV_W - 1):])
        for acc, val in zip(outs, new):
            acc.append(val)

    return (xp, xs.reshape(bs, ts, d)) + tuple(jnp.stack(o) for o in outs)
```
